```python
import math
import jax
import jax.numpy as jnp
from jax import lax
import numpy as np

D_MODEL = 2048
BATCH = 4
SEQ = 4096
DEPTH = 4
DEC_BATCH = 8
DEC_SEQ = 4096
PAST_LEN = 128

HEAD_DIM = 128
N_HEADS = D_MODEL // HEAD_DIM
A_Q_HEADS = N_HEADS // 4
A_KV_HEADS = A_Q_HEADS // 2
B_PAIRS = ((128, 1), (512, 4), (2048, 16))
B_HEADS_PER_PAIR = (N_HEADS - A_Q_HEADS) // 2 // len(B_PAIRS)
B_HEADS = B_HEADS_PER_PAIR * len(B_PAIRS)
C_HEADS = N_HEADS - A_Q_HEADS - B_HEADS
A_QW = A_Q_HEADS * HEAD_DIM
A_KVW = A_KV_HEADS * HEAD_DIM
B_W = B_HEADS * HEAD_DIM
C_W = C_HEADS * HEAD_DIM
MIX_W = A_QW + B_W + C_W
IN_W = A_QW + 2 * A_KVW + 3 * B_W + 3 * C_W
GRID_W = 64
NA_ROWS = 8
NA_COLS = 16
NA_QCOLS = 16
NA_KCOLS = 32
Q_BLOCK = 128
D_FF = 4 * D_MODEL
ROPE_THETA = 10000.0
EPS = 1e-6
NEG = -1e30
DEEPNORM_ALPHA = (2 * DEPTH) ** 0.25
DEEPNORM_BETA = (8 * DEPTH) ** -0.25

kernel_name = 'hybrid_bidir_encoder_gqa_dilated_natten'


def _rms_unit(x):
    xf = x.astype(jnp.float32)
    y = xf * lax.rsqrt(jnp.mean(xf * xf, axis=-1, keepdims=True) + EPS)
    return y.astype(x.dtype)


def _layernorm(x, g, b):
    xf = x.astype(jnp.float32)
    mu = jnp.mean(xf, axis=-1, keepdims=True)
    var = jnp.mean(jnp.square(xf - mu), axis=-1, keepdims=True)
    y = (xf - mu) * lax.rsqrt(var + EPS) * g.astype(jnp.float32) + b.astype(jnp.float32)
    return y.astype(x.dtype)


def _rope_angles(pos, dim):
    inv = ROPE_THETA ** (-(jnp.arange(dim // 2, dtype=jnp.float32) * 2.0 / dim))
    return pos.astype(jnp.float32)[:, None] * inv[None, :]


def _rope(x, ang):
    n = x.shape[-1] // 2
    xf = x.astype(jnp.float32)
    x1, x2 = xf[..., :n], xf[..., n:]
    c = jnp.cos(ang)[None, :, None, :]
    s = jnp.sin(ang)[None, :, None, :]
    return jnp.concatenate([x1 * c - x2 * s, x2 * c + x1 * s], axis=-1).astype(x.dtype)


def _axial_rope(x, row_ang, col_ang):
    half = HEAD_DIM // 2
    return jnp.concatenate([_rope(x[..., :half], row_ang), _rope(x[..., half:], col_ang)], axis=-1)


def _global_gqa(q, k, v):
    Bsz, S = q.shape[:2]
    rep = A_Q_HEADS // A_KV_HEADS
    scale = HEAD_DIM ** -0.5
    qb = q.reshape(Bsz, S // Q_BLOCK, Q_BLOCK, A_KV_HEADS, rep, HEAD_DIM)
    qb = jnp.moveaxis(qb, 1, 0)

    def block(qi):
        s = jnp.einsum('bqgrd,bkgd->bgrqk', qi, k, preferred_element_type=jnp.float32) * scale
        p = jax.nn.softmax(s, axis=-1)
        return jnp.einsum('bgrqk,bkgd->bqgrd', p.astype(v.dtype), v)

    o = lax.map(block, qb)
    return jnp.moveaxis(o, 0, 1).reshape(Bsz, S, A_QW)


def _dilated_window_attention(q, k, v, window, dilation):
    Bsz, S, H, D = q.shape
    n_side = (window // 2) // dilation
    L = S // dilation

    def strided(z):
        return z.reshape(Bsz, L, dilation, H, D).transpose(0, 2, 1, 3, 4)

    qs, ks, vs = strided(q), strided(k), strided(v)
    blk = math.gcd(L, Q_BLOCK)
    nb = L // blk
    kw = blk + 2 * n_side
    pad = ((0, 0), (0, 0), (n_side, n_side), (0, 0), (0, 0))
    kp = jnp.pad(ks, pad)
    vp = jnp.pad(vs, pad)
    starts = jnp.arange(nb) * blk
    idx = starts[:, None] + jnp.arange(kw)[None, :]
    kg = kp[:, :, idx]
    vg = vp[:, :, idx]
    qb = qs.reshape(Bsz, dilation, nb, blk, H, D)
    s = jnp.einsum('bznqhd,bznkhd->bznhqk', qb, kg, preferred_element_type=jnp.float32) * (HEAD_DIM ** -0.5)
    qi = jnp.arange(blk)[:, None]
    kj = jnp.arange(kw)[None, :]
    rel = kj - qi
    kpos = starts[:, None, None] + kj[None] - n_side
    valid = (rel >= 0)[None] & (rel <= 2 * n_side)[None] & (kpos >= 0) & (kpos < L)
    s = jnp.where(valid[None, None, :, None], s, NEG)
    lse = jax.nn.logsumexp(s, axis=-1)
    p = jnp.exp(s - lse[..., None])
    o = jnp.einsum('bznhqk,bznkhd->bznqhd', p.astype(v.dtype), vg)
    o = o.reshape(Bsz, dilation, L, H, D).transpose(0, 2, 1, 3, 4).reshape(Bsz, S, H, D)
    lse = lse.transpose(0, 1, 2, 4, 3).reshape(Bsz, dilation, L, H).transpose(0, 2, 1, 3).reshape(Bsz, S, H)
    return o, lse


def _dilated_mixture(q, k, v):
    Bsz, S = q.shape[:2]
    shp = (Bsz, S, len(B_PAIRS), B_HEADS_PER_PAIR, HEAD_DIM)
    q, k, v = q.reshape(shp), k.reshape(shp), v.reshape(shp)
    outs, lses = [], []
    for g, (window, dilation) in enumerate(B_PAIRS):
        o, lse = _dilated_window_attention(q[:, :, g], k[:, :, g], v[:, :, g], window, dilation)
        outs.append(o)
        lses.append(lse)
    o = jnp.stack(outs, axis=2)
    alpha = jax.nn.softmax(jnp.stack(lses, axis=2), axis=2)
    return (o.astype(jnp.float32) * alpha[..., None]).astype(q.dtype).reshape(Bsz, S, B_W)


def _neighbourhood_attention(q, k, v, rel_bias):
    Bsz, S, H, D = q.shape
    rows = S // GRID_W
    win_r = min(NA_ROWS, rows)
    n_cb = GRID_W // NA_QCOLS
    qg = q.reshape(Bsz, rows, n_cb, NA_QCOLS, H, D)
    kg = k.reshape(Bsz, rows, GRID_W, H, D)
    vg = v.reshape(Bsz, rows, GRID_W, H, D)
    cb = jnp.arange(n_cb)
    kc0 = jnp.clip(cb * NA_QCOLS - NA_COLS // 2, 0, GRID_W - NA_KCOLS)
    kcols = kc0[:, None] + jnp.arange(NA_KCOLS)[None, :]
    qcols = cb[:, None] * NA_QCOLS + jnp.arange(NA_QCOLS)[None, :]
    sc = jnp.clip(qcols - NA_COLS // 2, 0, GRID_W - NA_COLS)
    col_ok = (kcols[:, None, :] >= sc[..., None]) & (kcols[:, None, :] < sc[..., None] + NA_COLS)
    dcol = jnp.clip(kcols[:, None, :] - qcols[..., None] + NA_COLS - 1, 0, 2 * NA_COLS - 2)
    scale = HEAD_DIM ** -0.5

    def row_block(r):
        r0 = jnp.clip(r - win_r // 2, 0, rows - win_r)
        kr = lax.dynamic_slice_in_dim(kg, r0, win_r, axis=1)[:, :, kcols]
        vr = lax.dynamic_slice_in_dim(vg, r0, win_r, axis=1)[:, :, kcols]
        qr = lax.dynamic_index_in_dim(qg, r, axis=1, keepdims=False)
        s = jnp.einsum('bcqhd,bjckhd->bhcqjk', qr, kr, preferred_element_type=jnp.float32) * scale
        drow = r0 + jnp.arange(win_r) - r + NA_ROWS - 1
        bias = rel_bias[:, drow[None, None, :, None], dcol[:, :, None, :]]
        s = s + bias.astype(jnp.float32)[None]
        s = jnp.where(col_ok[None, None, :, :, None, :], s, NEG)
        p = jax.nn.softmax(s.reshape(s.shape[:4] + (win_r * NA_KCOLS,)), axis=-1).reshape(s.shape)
        return jnp.einsum('bhcqjk,bjckhd->bcqhd', p.astype(v.dtype), vr)

    o = lax.map(row_block, jnp.arange(rows))
    return jnp.moveaxis(o, 0, 1).reshape(Bsz, S, C_W)


def _layer(x, w_in, a_q_gain, a_k_gain, c_rel_bias, mix_gain, w_out, ln1_g, ln1_b, w_ff1, w_ff2, ln2_g, ln2_b):
    Bsz, S, _ = x.shape
    t = jnp.arange(S)
    sizes = [A_QW, A_KVW, A_KVW, B_W, B_W, B_W, C_W, C_W, C_W]
    splits = [int(c) for c in np.cumsum(sizes)[:-1]]
    proj = jnp.einsum('bsd,de->bse', x, w_in)
    aq, ak, av, bq, bk, bv, cq, ck, cv = jnp.split(proj, splits, axis=-1)

    def heads(z):
        return z.reshape(Bsz, S, -1, HEAD_DIM)

    row_ang = _rope_angles(t // GRID_W, HEAD_DIM // 2)
    col_ang = _rope_angles(t % GRID_W, HEAD_DIM // 2)
    qa = _axial_rope(_rms_unit(heads(aq)) * a_q_gain, row_ang, col_ang)
    ka = _axial_rope(_rms_unit(heads(ak)) * a_k_gain, row_ang, col_ang)
    o_a = _global_gqa(qa, ka, heads(av))
    ang = _rope_angles(t, HEAD_DIM)
    o_b = _dilated_mixture(_rope(heads(bq), ang), _rope(heads(bk), ang), heads(bv))
    o_c = _neighbourhood_attention(heads(cq), heads(ck), heads(cv), c_rel_bias)

    mixed = jnp.concatenate([_rms_unit(o_a), _rms_unit(o_b), _rms_unit(o_c)], axis=-1) * mix_gain
    y = jnp.einsum('bse,ed->bsd', mixed, w_out)
    x = _layernorm(DEEPNORM_ALPHA * x + y, ln1_g, ln1_b)
    h = jnp.einsum('bsf,fd->bsd', jnp.square(jax.nn.relu(jnp.einsum('bsd,df->bsf', x, w_ff1))), w_ff2)
    return _layernorm(DEEPNORM_ALPHA * x + h, ln2_g, ln2_b)


def _trunk(x, w_in, a_q_gain, a_k_gain, c_rel_bias, mix_gain, w_out, ln1_g, ln1_b, w_ff1, w_ff2, ln2_g, ln2_b):
    for l in range(DEPTH):
        x = _layer(x, w_in[l], a_q_gain[l], a_k_gain[l], c_rel_bias[l], mix_gain[l], w_out[l],
                   ln1_g[l], ln1_b[l], w_ff1[l], w_ff2[l], ln2_g[l], ln2_b[l])
    return x


def setup_inputs(seed: int = 0) -> dict:
    key = jax.random.key(seed)
    ks = jax.random.split(key, 15)
    nrm = jax.random.normal
    f32 = jnp.float32
    return {
        'x_prompt': nrm(ks[0], (BATCH, SEQ, D_MODEL), f32),
        'x_sample': nrm(ks[1], (DEC_BATCH, DEC_SEQ, D_MODEL), f32),
        'w_in': nrm(ks[2], (DEPTH, D_MODEL, IN_W), f32) * D_MODEL ** -0.5,
        'a_q_gain': 1.0 + 0.02 * nrm(ks[3], (DEPTH, HEAD_DIM), f32),
        'a_k_gain': 1.0 + 0.02 * nrm(ks[4], (DEPTH, HEAD_DIM), f32),
        'c_rel_bias': 0.02 * nrm(ks[5], (DEPTH, C_HEADS, 2 * NA_ROWS - 1, 2 * NA_COLS - 1), f32),
        'mix_gain': 1.0 + 0.02 * nrm(ks[6], (DEPTH, MIX_W), f32),
        'w_out': nrm(ks[7], (DEPTH, MIX_W, D_MODEL), f32) * (MIX_W ** -0.5 * DEEPNORM_BETA),
        'ln1_g': 1.0 + 0.02 * nrm(ks[8], (DEPTH, D_MODEL), f32),
        'ln1_b': 0.02 * nrm(ks[9], (DEPTH, D_MODEL), f32),
        'w_ff1': nrm(ks[10], (DEPTH, D_MODEL, D_FF), f32) * D_MODEL ** -0.5,
        'w_ff2': nrm(ks[11], (DEPTH, D_FF, D_MODEL), f32) * (D_FF ** -0.5 * DEEPNORM_BETA),
        'ln2_g': 1.0 + 0.02 * nrm(ks[12], (DEPTH, D_MODEL), f32),
        'ln2_b': 0.02 * nrm(ks[13], (DEPTH, D_MODEL), f32),
    }


def reference(x_prompt, x_sample, w_in, a_q_gain, a_k_gain, c_rel_bias, mix_gain, w_out,
              ln1_g, ln1_b, w_ff1, w_ff2, ln2_g, ln2_b):
    y_prompt = _trunk(x_prompt, w_in, a_q_gain, a_k_gain, c_rel_bias, mix_gain, w_out,
                      ln1_g, ln1_b, w_ff1, w_ff2, ln2_g, ln2_b)
    y_sample = _trunk(x_sample, w_in, a_q_gain, a_k_gain, c_rel_bias, mix_gain, w_out,
                      ln1_g, ln1_b, w_ff1, w_ff2, ln2_g, ln2_b)
    return (y_prompt, y_sample)
```

```python
import functools
import math

import jax
import jax.numpy as jnp
from jax import lax
from jax.experimental import pallas as pl
from jax.experimental.pallas import tpu as pltpu

F32 = jnp.float32
BF16 = jnp.bfloat16

HEAD_DIM = 128
A_Q_HEADS = 4
A_KV_HEADS = 2
B_PAIRS = ((128, 1), (512, 4), (2048, 16))
B_HEADS_PER_PAIR = 2
B_HEADS = 6
C_HEADS = 6
GRID_W = 64
NA_ROWS = 8
NA_COLS = 16
ROPE_THETA = 10000.0
EPS = 1e-6
NEG = -1e30
SCALE = HEAD_DIM ** -0.5

COL_AQ, COL_AK, COL_AV = 0, 4, 6
COL_BQ, COL_BK, COL_BV = 8, 14, 20
COL_CQ, COL_CK, COL_CV = 26, 32, 38
N_COLS = 44

B_BLOCK = 128
B_SIDE = 64
B_KWIN = B_BLOCK + 2 * B_SIDE
C_QROWS = 4
C_KROWS = 12

VMEM_LIMIT = 56 * 1024 * 1024


def _cparams(sem):
    return pltpu.CompilerParams(dimension_semantics=sem, vmem_limit_bytes=VMEM_LIMIT)


def _inproj_kernel(x_ref, w_ref, cs_ref, gq_ref, gk_ref, cax_ref, s1_ref, s2_ref, cb_ref, sb_ref,
                   o_ref, xbf_ref, *, tn):
    xbf_ref[...] = x_ref[...].astype(BF16)
    heads_per_tile = tn // HEAD_DIM

    def axial(xh, gain):
        ms = jnp.mean(xh * xh, axis=-1, keepdims=True)
        y = xh * lax.rsqrt(ms + EPS) * gain
        return y * cax_ref[...] + pltpu.roll(y, 96, 1) * s1_ref[...] + pltpu.roll(y, 32, 1) * s2_ref[...]

    def rope(xh):
        return xh * cb_ref[...] + pltpu.roll(xh, 64, 1) * sb_ref[...]

    for t in range(N_COLS // heads_per_tile):
        acc = jnp.dot(xbf_ref[...], w_ref[:, t * tn:(t + 1) * tn], preferred_element_type=F32)
        acc = acc * cs_ref[:, t * tn:(t + 1) * tn]
        for c in range(heads_per_tile):
            col = t * heads_per_tile + c
            xh = acc[:, c * HEAD_DIM:(c + 1) * HEAD_DIM]
            if COL_AQ <= col < COL_AK:
                xh = axial(xh, gq_ref[...])
            elif COL_AK <= col < COL_AV:
                xh = axial(xh, gk_ref[...])
            elif COL_BQ <= col < COL_BV:
                xh = rope(xh)
            o_ref[col] = xh.astype(BF16)


def _inproj(x, w_in, l, colscale, gq, gk, tabs, *, tm=512, tn=512):
    M, D = x.shape
    n_pos_blocks = tabs[0].shape[0] // tm
    tab_spec = pl.BlockSpec((tm, HEAD_DIM), lambda i: (i % n_pos_blocks, 0))
    row = lambda n: pl.BlockSpec((1, n), lambda i: (0, 0))
    return pl.pallas_call(
        functools.partial(_inproj_kernel, tn=tn),
        grid=(M // tm,),
        in_specs=[
            pl.BlockSpec((tm, D), lambda i: (i, 0)),
            pl.BlockSpec((None, D, N_COLS * HEAD_DIM), lambda i: (l, 0, 0), pipeline_mode=pl.Buffered(1)),
            row(N_COLS * HEAD_DIM),
            pl.BlockSpec((None, 1, HEAD_DIM), lambda i: (l, 0, 0)),
            pl.BlockSpec((None, 1, HEAD_DIM), lambda i: (l, 0, 0)),
            tab_spec, tab_spec, tab_spec, tab_spec, tab_spec,
        ],
        out_specs=pl.BlockSpec((N_COLS, tm, HEAD_DIM), lambda i: (0, i, 0)),
        out_shape=jax.ShapeDtypeStruct((N_COLS, M, HEAD_DIM), BF16),
        scratch_shapes=[pltpu.VMEM((tm, D), BF16)],
        compiler_params=_cparams(("parallel",)),
        name="inproj",
    )(x, w_in, colscale, gq, gk, *tabs)


def _gqa_kernel(q_ref, k_ref, v_ref, o_ref):
    rep, tq, _ = q_ref.shape
    q = q_ref[...].reshape(rep * tq, HEAD_DIM)
    s = lax.dot_general(q, k_ref[0], (((1,), (1,)), ((), ())), preferred_element_type=F32)
    m = jnp.max(s, axis=-1, keepdims=True)
    p = jnp.exp(s - m)
    l = jnp.sum(p, axis=-1, keepdims=True)
    o = jnp.dot(p.astype(BF16), v_ref[0], preferred_element_type=F32) * (1.0 / l)
    o_ref[...] = o.reshape(rep, tq, HEAD_DIM).astype(BF16)


def _gqa(proj, batch, seq, *, tq=256):
    M = batch * seq
    rep = A_Q_HEADS // A_KV_HEADS
    nq = seq // tq
    return pl.pallas_call(
        _gqa_kernel,
        grid=(batch, A_KV_HEADS, nq),
        in_specs=[
            pl.BlockSpec((rep, tq, HEAD_DIM), lambda b, g, i: (g, b * nq + i, 0)),
            pl.BlockSpec((1, seq, HEAD_DIM), lambda b, g, i: (COL_AK + g, b, 0)),
            pl.BlockSpec((1, seq, HEAD_DIM), lambda b, g, i: (COL_AV + g, b, 0)),
        ],
        out_specs=pl.BlockSpec((rep, tq, HEAD_DIM), lambda b, g, i: (g, b * nq + i, 0)),
        out_shape=jax.ShapeDtypeStruct((A_Q_HEADS, M, HEAD_DIM), BF16),
        compiler_params=_cparams(("parallel", "parallel", "arbitrary")),
        name="gqa",
    )(proj, proj, proj)


def _dilated_kernel(q_ref, k_ref, v_ref, o_ref, lse_ref, *, dilation, length):
    n_blocks = length // B_BLOCK
    rel0 = (lax.broadcasted_iota(jnp.int32, (B_BLOCK, B_KWIN), 1)
            - lax.broadcasted_iota(jnp.int32, (B_BLOCK, B_KWIN), 0))

    def block(qb, r):
        lanes = slice(r * HEAD_DIM, (r + 1) * HEAD_DIM)
        if isinstance(qb, int):
            q0 = qb * B_BLOCK
            k0 = min(max(q0 - B_SIDE, 0), length - B_KWIN)
        else:
            q0 = pl.multiple_of(qb * B_BLOCK, B_BLOCK)
            k0 = pl.multiple_of(jnp.clip(q0 - B_SIDE, 0, length - B_KWIN), B_SIDE)
        q = q_ref[0, 0, pl.ds(q0, B_BLOCK), lanes]
        k = k_ref[0, 0, pl.ds(k0, B_KWIN), lanes]
        v = v_ref[0, 0, pl.ds(k0, B_KWIN), lanes]
        s = lax.dot_general(q, k, (((1,), (1,)), ((), ())), preferred_element_type=F32)
        rel = rel0 + (k0 - q0)
        s = jnp.where((rel >= -B_SIDE) & (rel <= B_SIDE), s, NEG)
        m = jnp.max(s, axis=-1, keepdims=True)
        p = jnp.exp(s - m)
        l = jnp.sum(p, axis=-1, keepdims=True)
        o = jnp.dot(p.astype(BF16), v, preferred_element_type=F32) * (1.0 / l)
        o_ref[0, 0, pl.ds(q0, B_BLOCK), lanes] = o.astype(BF16)
        lse_ref[0, 0, pl.ds(q0, B_BLOCK), lanes] = jnp.broadcast_to(m + jnp.log(l), (B_BLOCK, HEAD_DIM))

    for r in range(dilation):
        if n_blocks <= 2:
            for qb in range(n_blocks):
                block(qb, r)
        else:
            def body(qb, carry, r=r):
                block(qb, r)
                return carry
            lax.fori_loop(0, n_blocks, body, 0)


def _dilated(proj, batch, seq, pair):
    _, dilation = B_PAIRS[pair]
    length = seq // dilation
    width = dilation * HEAD_DIM
    view = proj.reshape(N_COLS, batch, length, width)
    hpp = B_HEADS_PER_PAIR

    def spec(col0):
        return pl.BlockSpec((1, 1, length, width), lambda b, h: (col0 + pair * hpp + h, b, 0, 0))

    out_spec = pl.BlockSpec((1, 1, length, width), lambda b, h: (h, b, 0, 0))
    o, lse = pl.pallas_call(
        functools.partial(_dilated_kernel, dilation=dilation, length=length),
        grid=(batch, hpp),
        in_specs=[spec(COL_BQ), spec(COL_BK), spec(COL_BV)],
        out_specs=[out_spec, out_spec],
        out_shape=[jax.ShapeDtypeStruct((hpp, batch, length, width), BF16),
                   jax.ShapeDtypeStruct((hpp, batch, length, width), F32)],
        compiler_params=_cparams(("parallel", "parallel")),
        name=f"dilated{dilation}",
    )(view, view, view)
    M = batch * seq
    return o.reshape(hpp, M, HEAD_DIM), lse.reshape(hpp, M, HEAD_DIM)


def _natten_kernel(q_ref, k_ref, v_ref, p_ref, o_ref, *, rows):
    n_blocks = rows // C_QROWS
    tq = C_QROWS * GRID_W
    tk = C_KROWS * GRID_W
    lane = lax.broadcasted_iota(jnp.int32, (GRID_W, 2 * GRID_W), 1)

    def body(i, carry):
        r_first = i * C_QROWS
        w0 = jnp.clip(r_first - NA_ROWS // 2, 0, rows - C_KROWS)
        q0 = pl.multiple_of(i * tq, tq)
        k0 = pl.multiple_of(w0 * GRID_W, GRID_W)
        q = q_ref[0, pl.ds(q0, tq), :]
        k = k_ref[0, pl.ds(k0, tk), :]
        v = v_ref[0, pl.ds(k0, tk), :]
        s = lax.dot_general(q, k, (((1,), (1,)), ((), ())), preferred_element_type=F32)
        bias_rows = []
        for a in range(C_QROWS):
            r = r_first + a
            r0 = jnp.clip(r - NA_ROWS // 2, 0, rows - NA_ROWS)
            tiles = []
            for jp in range(C_KROWS // 2):
                kr = w0 + 2 * jp
                drow = kr - r + NA_ROWS - 1
                slab = p_ref[0, pl.ds(jnp.clip(drow + 1, 0, 2 * NA_ROWS - 1), 1)][0]
                left_ok = (kr >= r0) & (kr < r0 + NA_ROWS)
                right_ok = (kr + 1 >= r0) & (kr + 1 < r0 + NA_ROWS)
                lo = jnp.where(left_ok, 0, GRID_W)
                hi = jnp.where(right_ok, 2 * GRID_W, GRID_W)
                tiles.append(jnp.where((lane >= lo) & (lane < hi), slab, NEG))
            bias_rows.append(jnp.concatenate(tiles, axis=1))
        s = s + jnp.concatenate(bias_rows, axis=0)
        m = jnp.max(s, axis=-1, keepdims=True)
        p = jnp.exp(s - m)
        l = jnp.sum(p, axis=-1, keepdims=True)
        o = jnp.dot(p.astype(BF16), v, preferred_element_type=F32) * (1.0 / l)
        o_ref[0, pl.ds(q0, tq), :] = o.astype(BF16)
        return carry

    lax.fori_loop(0, n_blocks, body, 0)


def _natten(proj, bias_pairs, l, batch, seq):
    M = batch * seq
    rows = seq // GRID_W

    def spec(col0):
        return pl.BlockSpec((1, seq, HEAD_DIM), lambda b, h: (col0 + h, b, 0))

    return pl.pallas_call(
        functools.partial(_natten_kernel, rows=rows),
        grid=(batch, C_HEADS),
        in_specs=[spec(COL_CQ), spec(COL_CK), spec(COL_CV),
                  pl.BlockSpec((None, 1, 2 * NA_ROWS, GRID_W, 2 * GRID_W), lambda b, h: (l, h, 0, 0, 0))],
        out_specs=pl.BlockSpec((1, seq, HEAD_DIM), lambda b, h: (h, b, 0)),
        out_shape=jax.ShapeDtypeStruct((C_HEADS, M, HEAD_DIM), BF16),
        compiler_params=_cparams(("parallel", "parallel")),
        name="natten",
    )(proj, proj, proj, bias_pairs)


def _layernorm(z, g, b):
    mu = jnp.mean(z, axis=-1, keepdims=True)
    zc = z - mu
    var = jnp.mean(zc * zc, axis=-1, keepdims=True)
    return zc * lax.rsqrt(var + EPS) * g + b


def _outproj_kernel(oa_ref, ob0_ref, ob1_ref, ob2_ref, l0_ref, l1_ref, l2_ref, oc_ref, x_ref, w_ref,
                    mg_ref, g_ref, b_ref, y_ref, mixed_ref, *, alpha):
    def put_group(tiles, col0):
        ss = sum(jnp.sum(t * t, axis=-1, keepdims=True) for t in tiles)
        r = lax.rsqrt(ss * (1.0 / (len(tiles) * HEAD_DIM)) + EPS)
        for h, t in enumerate(tiles):
            lanes = slice((col0 + h) * HEAD_DIM, (col0 + h + 1) * HEAD_DIM)
            mixed_ref[:, lanes] = (t * r * mg_ref[:, lanes]).astype(BF16)

    put_group([oa_ref[h].astype(F32) for h in range(A_Q_HEADS)], 0)

    ob_refs = (ob0_ref, ob1_ref, ob2_ref)
    lse_refs = (l0_ref, l1_ref, l2_ref)
    b_tiles = [None] * B_HEADS
    for h in range(B_HEADS_PER_PAIR):
        lses = [ref[h] for ref in lse_refs]
        mx = jnp.maximum(jnp.maximum(lses[0], lses[1]), lses[2])
        es = [jnp.exp(x - mx) for x in lses]
        inv = 1.0 / (es[0] + es[1] + es[2])
        for g in range(len(B_PAIRS)):
            b_tiles[g * B_HEADS_PER_PAIR + h] = ob_refs[g][h].astype(F32) * (es[g] * inv)
    put_group(b_tiles, A_Q_HEADS)

    put_group([oc_ref[h].astype(F32) for h in range(C_HEADS)], A_Q_HEADS + B_HEADS)

    y = jnp.dot(mixed_ref[...], w_ref[...], preferred_element_type=F32)
    y_ref[...] = _layernorm(alpha * x_ref[...] + y, g_ref[...], b_ref[...])


def _outproj(oa, obs, lses, oc, x, w_out, mix_gain, ln_g, ln_b, l, alpha, *, tm=512):
    M, D = x.shape
    heads = lambda n: pl.BlockSpec((n, tm, HEAD_DIM), lambda i: (0, i, 0))
    vec = pl.BlockSpec((None, 1, D), lambda i: (l, 0, 0))
    return pl.pallas_call(
        functools.partial(_outproj_kernel, alpha=alpha),
        grid=(M // tm,),
        in_specs=[heads(A_Q_HEADS)] + [heads(B_HEADS_PER_PAIR)] * 6 + [heads(C_HEADS)] + [
            pl.BlockSpec((tm, D), lambda i: (i, 0)),
            pl.BlockSpec((None, D, D), lambda i: (l, 0, 0), pipeline_mode=pl.Buffered(1)),
            vec, vec, vec,
        ],
        out_specs=pl.BlockSpec((tm, D), lambda i: (i, 0)),
        out_shape=jax.ShapeDtypeStruct((M, D), F32),
        scratch_shapes=[pltpu.VMEM((tm, D), BF16)],
        compiler_params=_cparams(("parallel",)),
        name="outproj",
    )(oa, *obs, *lses, oc, x, w_out, mix_gain, ln_g, ln_b)


def _mlp_kernel(x_ref, w1_ref, w2_ref, g_ref, b_ref, y_ref, xbf_ref, acc_ref, *, alpha):
    k = pl.program_id(1)

    @pl.when(k == 0)
    def _():
        xbf_ref[...] = x_ref[...].astype(BF16)
        acc_ref[...] = jnp.zeros_like(acc_ref)

    h = jnp.dot(xbf_ref[...], w1_ref[...], preferred_element_type=F32)
    h = jnp.square(jnp.maximum(h, 0.0))
    acc_ref[...] += jnp.dot(h.astype(BF16), w2_ref[...], preferred_element_type=F32)

    @pl.when(k == pl.num_programs(1) - 1)
    def _():
        y_ref[...] = _layernorm(alpha * x_ref[...] + acc_ref[...], g_ref[...], b_ref[...])


def _mlp(x, w1, w2, ln_g, ln_b, l, alpha, *, tm=512, tf=1024):
    M, D = x.shape
    F = w1.shape[-1]
    vec = pl.BlockSpec((None, 1, D), lambda i, k: (l, 0, 0))
    return pl.pallas_call(
        functools.partial(_mlp_kernel, alpha=alpha),
        grid=(M // tm, F // tf),
        in_specs=[
            pl.BlockSpec((tm, D), lambda i, k: (i, 0)),
            pl.BlockSpec((None, D, tf), lambda i, k: (l, 0, k)),
            pl.BlockSpec((None, tf, D), lambda i, k: (l, k, 0)),
            vec, vec,
        ],
        out_specs=pl.BlockSpec((tm, D), lambda i, k: (i, 0)),
        out_shape=jax.ShapeDtypeStruct((M, D), F32),
        scratch_shapes=[pltpu.VMEM((tm, D), BF16), pltpu.VMEM((tm, D), F32)],
        compiler_params=_cparams(("parallel", "arbitrary")),
        name="mlp",
    )(x, w1, w2, ln_g, ln_b)


def _rope_tables(seq):
    t = jnp.arange(seq)

    def angles(pos, dim):
        inv = ROPE_THETA ** (-(jnp.arange(dim // 2, dtype=F32) * 2.0 / dim))
        return pos.astype(F32)[:, None] * inv[None, :]

    half = HEAD_DIM // 2
    ra, ca = angles(t // GRID_W, half), angles(t % GRID_W, half)
    zero = jnp.zeros_like(ra)
    cax = jnp.concatenate([jnp.cos(ra), jnp.cos(ra), jnp.cos(ca), jnp.cos(ca)], axis=-1)
    s1 = jnp.concatenate([-jnp.sin(ra), zero, -jnp.sin(ca), zero], axis=-1)
    s2 = jnp.concatenate([zero, jnp.sin(ra), zero, jnp.sin(ca)], axis=-1)
    ang = angles(t, HEAD_DIM)
    cb = jnp.concatenate([jnp.cos(ang), jnp.cos(ang)], axis=-1)
    sb = jnp.concatenate([-jnp.sin(ang), jnp.sin(ang)], axis=-1)
    return cax, s1, s2, cb, sb


def _bias_pairs(c_rel_bias):
    qc = jnp.arange(GRID_W)[:, None]
    kc = jnp.arange(GRID_W)[None, :]
    start = jnp.clip(qc - NA_COLS // 2, 0, GRID_W - NA_COLS)
    ok = (kc >= start) & (kc < start + NA_COLS)
    dcol = jnp.clip(kc - qc + NA_COLS - 1, 0, 2 * NA_COLS - 2)
    tile = jnp.where(ok[None, None, None], c_rel_bias[:, :, :, dcol], NEG)
    pad = jnp.full(tile.shape[:2] + (1,) + tile.shape[3:], NEG, F32)
    ext = jnp.concatenate([pad, tile, pad], axis=2)
    return jnp.concatenate([ext[:, :, :-1], ext[:, :, 1:]], axis=-1)


def _column_scale():
    cs = jnp.ones((N_COLS, HEAD_DIM), F32)
    for c0, n in ((COL_BQ, B_HEADS), (COL_CQ, C_HEADS)):
        cs = cs.at[c0:c0 + n].set(SCALE)
    return cs.reshape(1, N_COLS * HEAD_DIM)


def _trunk(x, params, tabs, colscale, depth, alpha):
    batch, seq, d_model = x.shape
    x = x.reshape(batch * seq, d_model)
    for l in range(depth):
        proj = _inproj(x, params["w_in"], l, colscale, params["gq"], params["gk"], tabs)
        oa = _gqa(proj, batch, seq)
        obs, lses = zip(*[_dilated(proj, batch, seq, pair) for pair in range(len(B_PAIRS))])
        oc = _natten(proj, params["bias_pairs"], l, batch, seq)
        x = _outproj(oa, obs, lses, oc, x, params["w_out"], params["mix_gain"], params["ln1_g"],
                     params["ln1_b"], l, alpha)
        x = _mlp(x, params["w_ff1"], params["w_ff2"], params["ln2_g"], params["ln2_b"], l, alpha)
    return x.reshape(batch, seq, d_model)


def kernel(x_prompt, x_sample, w_in, a_q_gain, a_k_gain, c_rel_bias, mix_gain, w_out, ln1_g, ln1_b,
           w_ff1, w_ff2, ln2_g, ln2_b):
    depth = w_in.shape[0]
    alpha = (2 * depth) ** 0.25
    seq = x_prompt.shape[1]
    vec = lambda a: a.reshape(depth, 1, a.shape[-1])
    params = {
        "w_in": w_in.astype(BF16), "w_out": w_out.astype(BF16),
        "w_ff1": w_ff1.astype(BF16), "w_ff2": w_ff2.astype(BF16),
        "gq": vec(a_q_gain * SCALE), "gk": vec(a_k_gain),
        "bias_pairs": _bias_pairs(c_rel_bias),
        "mix_gain": vec(mix_gain), "ln1_g": vec(ln1_g), "ln1_b": vec(ln1_b),
        "ln2_g": vec(ln2_g), "ln2_b": vec(ln2_b),
    }
    tabs = _rope_tables(seq)
    colscale = _column_scale()
    y_prompt = _trunk(x_prompt, params, tabs, colscale, depth, alpha)
    y_sample = _trunk(x_sample, params, tabs, colscale, depth, alpha)
    return (y_prompt, y_sample)
```

```python
import functools

import jax
import jax.numpy as jnp
from jax import lax
from jax.experimental import pallas as pl
from jax.experimental.pallas import tpu as pltpu

F32 = jnp.float32
BF16 = jnp.bfloat16

HEAD_DIM = 128
A_Q_HEADS = 4
A_KV_HEADS = 2
B_DILATIONS = (1, 4, 16)
B_HEADS_PER_PAIR = 2
B_HEADS = 6
C_HEADS = 6
GRID_W = 64
NA_ROWS = 8
NA_COLS = 16
ROPE_THETA = 10000.0
EPS = 1e-6
NEG = -1e30
SCALE = HEAD_DIM ** -0.5

COL_AQ, COL_AK, COL_AV = 0, 4, 6
COL_BQ, COL_BK, COL_BV = 8, 14, 20
COL_CQ, COL_CK, COL_CV = 26, 32, 38
N_COLS = 44

MAIN_AQ, MAIN_AK, MAIN_AV = 0, 4, 6
MAIN_BQ, MAIN_BK, MAIN_BV = 8, 10, 12
MAIN_CQ, MAIN_CK, MAIN_CV = 14, 20, 26
N_MAIN = 32
N_PAIR_COLS = 3 * B_HEADS_PER_PAIR


def _column_destination(col):
    hpp = B_HEADS_PER_PAIR
    for ref0, main0, n in ((COL_AQ, MAIN_AQ, 4), (COL_AK, MAIN_AK, 2), (COL_AV, MAIN_AV, 2),
                           (COL_CQ, MAIN_CQ, 6), (COL_CK, MAIN_CK, 6), (COL_CV, MAIN_CV, 6)):
        if ref0 <= col < ref0 + n:
            return 0, main0 + col - ref0
    for which, (ref0, main0) in enumerate(((COL_BQ, MAIN_BQ), (COL_BK, MAIN_BK), (COL_BV, MAIN_BV))):
        if ref0 <= col < ref0 + B_HEADS:
            pair, h = divmod(col - ref0, hpp)
            if pair == 0:
                return 0, main0 + h
            return pair, which * hpp + h
    raise ValueError(col)


B_BLOCK = 128
B_SIDE = 64
B_KWIN = B_BLOCK + 2 * B_SIDE
C_QROWS = 4
C_KROWS = 12
A_ROWS = 256
A_KCHUNK = 512

VMEM_LIMIT = 56 * 1024 * 1024


def _cparams(sem):
    return pltpu.CompilerParams(dimension_semantics=sem, vmem_limit_bytes=VMEM_LIMIT)


def _inproj_kernel(x_ref, w_ref, cs_ref, gq_ref, gk_ref, cax_ref, s1_ref, s2_ref, cb_ref, sb_ref,
                   main_ref, d4_ref, d16_ref, xbf_ref, perm_ref, *, tn):
    xbf_ref[...] = x_ref[...].astype(BF16)
    heads_per_tile = tn // HEAD_DIM
    tm = x_ref.shape[0]
    pair_refs = {1: d4_ref, 2: d16_ref}
    n_perm = 0

    def axial(xh, gain):
        ms = jnp.mean(xh * xh, axis=-1, keepdims=True)
        y = xh * lax.rsqrt(ms + EPS) * gain
        return y * cax_ref[...] + pltpu.roll(y, 96, 1) * s1_ref[...] + pltpu.roll(y, 32, 1) * s2_ref[...]

    def rope(xh):
        return xh * cb_ref[...] + pltpu.roll(xh, 64, 1) * sb_ref[...]

    for t in range(N_COLS // heads_per_tile):
        acc = jnp.dot(xbf_ref[...], w_ref[:, t * tn:(t + 1) * tn], preferred_element_type=F32)
        acc = acc * cs_ref[:, t * tn:(t + 1) * tn]
        for c in range(heads_per_tile):
            col = t * heads_per_tile + c
            xh = acc[:, c * HEAD_DIM:(c + 1) * HEAD_DIM]
            if COL_AQ <= col < COL_AK:
                xh = axial(xh, gq_ref[...])
            elif COL_AK <= col < COL_AV:
                xh = axial(xh, gk_ref[...])
            elif COL_BQ <= col < COL_BV:
                xh = rope(xh)
            dest, idx = _column_destination(col)
            if dest == 0:
                main_ref[idx] = xh.astype(BF16)
            else:
                d = B_DILATIONS[dest]
                slot = n_perm % perm_ref.shape[0]
                n_perm += 1
                perm_ref[slot] = xh
                for r in range(d):
                    pair_refs[dest][idx, 0, r] = perm_ref[slot, pl.ds(r, tm // d, stride=d), :].astype(BF16)


def _inproj(x, w_in, l, colscale, gq, gk, tabs, batch, seq, *, tm=512, tn=512):
    M, D = x.shape
    nblk = seq // tm
    tab_spec = pl.BlockSpec((tm, HEAD_DIM), lambda i: (i % nblk, 0))

    def pair_out(d):
        shape = (N_PAIR_COLS, batch, d, seq // d, HEAD_DIM)
        spec = pl.BlockSpec((N_PAIR_COLS, 1, d, tm // d, HEAD_DIM), lambda i: (0, i // nblk, 0, i % nblk, 0))
        return jax.ShapeDtypeStruct(shape, BF16), spec

    (d4_shape, d4_spec), (d16_shape, d16_spec) = pair_out(B_DILATIONS[1]), pair_out(B_DILATIONS[2])
    return pl.pallas_call(
        functools.partial(_inproj_kernel, tn=tn),
        grid=(M // tm,),
        in_specs=[
            pl.BlockSpec((tm, D), lambda i: (i, 0)),
            pl.BlockSpec((None, D, N_COLS * HEAD_DIM), lambda i: (l, 0, 0), pipeline_mode=pl.Buffered(1)),
            pl.BlockSpec((1, N_COLS * HEAD_DIM), lambda i: (0, 0)),
            pl.BlockSpec((None, 1, HEAD_DIM), lambda i: (l, 0, 0)),
            pl.BlockSpec((None, 1, HEAD_DIM), lambda i: (l, 0, 0)),
            tab_spec, tab_spec, tab_spec, tab_spec, tab_spec,
        ],
        out_specs=[pl.BlockSpec((N_MAIN, tm, HEAD_DIM), lambda i: (0, i, 0)), d4_spec, d16_spec],
        out_shape=[jax.ShapeDtypeStruct((N_MAIN, M, HEAD_DIM), BF16), d4_shape, d16_shape],
        scratch_shapes=[pltpu.VMEM((tm, D), BF16), pltpu.VMEM((2, tm, HEAD_DIM), F32)],
        compiler_params=_cparams(("parallel",)),
        name="inproj",
    )(x, w_in, colscale, gq, gk, *tabs)


def _lane_groups(x):
    return [x[:, g * HEAD_DIM:(g + 1) * HEAD_DIM] for g in range(x.shape[1] // HEAD_DIM)]


def _gqa_kernel(q_ref, k_ref, v_ref, o_ref, s_ref, m_ref):
    rep, seq, _ = q_ref.shape
    blocks_per_head = seq // A_ROWS
    n_blocks = rep * blocks_per_head
    n_chunks = seq // A_KCHUNK
    chunks = [slice(c * A_KCHUNK, (c + 1) * A_KCHUNK) for c in range(n_chunks)]

    def rows_of(i):
        if isinstance(i, int):
            return i // blocks_per_head, pl.ds((i % blocks_per_head) * A_ROWS, A_ROWS)
        return i // blocks_per_head, pl.ds(pl.multiple_of((i % blocks_per_head) * A_ROWS, A_ROWS), A_ROWS)

    def scores(i, slot):
        h, rows = rows_of(i)
        q = q_ref[h, rows, :]
        mx = None
        for keys in chunks:
            s = lax.dot_general(q, k_ref[0, keys, :], (((1,), (1,)), ((), ())), preferred_element_type=F32)
            s_ref[slot, :, keys] = s
            part = functools.reduce(jnp.maximum, _lane_groups(s))
            mx = part if mx is None else jnp.maximum(mx, part)
        m_ref[slot] = jnp.broadcast_to(jnp.max(mx, axis=-1, keepdims=True), (A_ROWS, HEAD_DIM))

    def finish(i, slot):
        h, rows = rows_of(i)
        m = m_ref[slot]
        acc = jnp.zeros((A_ROWS, HEAD_DIM), F32)
        lsum = jnp.zeros((A_ROWS, HEAD_DIM), F32)
        for keys in chunks:
            ps = [jnp.exp(s - m) for s in _lane_groups(s_ref[slot, :, keys])]
            lsum = lsum + functools.reduce(jnp.add, ps)
            p = jnp.concatenate(ps, axis=1).astype(BF16)
            acc = acc + jnp.dot(p, v_ref[0, keys, :], preferred_element_type=F32)
        l = jnp.sum(lsum, axis=-1, keepdims=True)
        o_ref[h, rows, :] = (acc * (1.0 / l)).astype(BF16)

    scores(0, 0)

    def body(j, carry):
        scores(2 * j + 1, 1)
        finish(2 * j, 0)
        scores(2 * j + 2, 0)
        finish(2 * j + 1, 1)
        return carry

    lax.fori_loop(0, n_blocks // 2 - 1, body, 0)
    scores(n_blocks - 1, 1)
    finish(n_blocks - 2, 0)
    finish(n_blocks - 1, 1)


def _gqa(main, batch, seq):
    M = batch * seq
    rep = A_Q_HEADS // A_KV_HEADS
    return pl.pallas_call(
        _gqa_kernel,
        grid=(batch, A_KV_HEADS),
        in_specs=[
            pl.BlockSpec((rep, seq, HEAD_DIM), lambda b, g: (g, b, 0)),
            pl.BlockSpec((1, seq, HEAD_DIM), lambda b, g: (MAIN_AK + g, b, 0)),
            pl.BlockSpec((1, seq, HEAD_DIM), lambda b, g: (MAIN_AV + g, b, 0)),
        ],
        out_specs=pl.BlockSpec((rep, seq, HEAD_DIM), lambda b, g: (g, b, 0)),
        out_shape=jax.ShapeDtypeStruct((A_Q_HEADS, M, HEAD_DIM), BF16),
        scratch_shapes=[pltpu.VMEM((2, A_ROWS, seq), F32), pltpu.VMEM((2, A_ROWS, HEAD_DIM), F32)],
        compiler_params=_cparams(("parallel", "parallel")),
        name="gqa",
    )(main, main, main)


def _dilated_kernel(q_ref, k_ref, v_ref, o_ref, lse_ref, *, dilation, length):
    rel0 = (lax.broadcasted_iota(jnp.int32, (B_BLOCK, B_KWIN), 1)
            - lax.broadcasted_iota(jnp.int32, (B_BLOCK, B_KWIN), 0))
    masks = {}
    for r in range(dilation):
        for qb in range(length // B_BLOCK):
            q0 = qb * B_BLOCK
            k0 = min(max(q0 - B_SIDE, 0), length - B_KWIN)
            if k0 - q0 not in masks:
                rel = rel0 + (k0 - q0)
                masks[k0 - q0] = (rel >= -B_SIDE) & (rel <= B_SIDE)
            q_rows = pl.ds(r * length + q0, B_BLOCK)
            k_rows = pl.ds(r * length + k0, B_KWIN)
            s = lax.dot_general(q_ref[0, 0, q_rows, :], k_ref[0, 0, k_rows, :], (((1,), (1,)), ((), ())),
                                preferred_element_type=F32)
            s = jnp.where(masks[k0 - q0], s, NEG)
            m = jnp.max(s, axis=-1, keepdims=True)
            p = jnp.exp(s - m)
            l = jnp.sum(p, axis=-1, keepdims=True)
            o = jnp.dot(p.astype(BF16), v_ref[0, 0, k_rows, :], preferred_element_type=F32) * (1.0 / l)
            o_ref[0, 0, q_rows, :] = o.astype(BF16)
            lse_ref[0, 0, q_rows, :] = jnp.broadcast_to(m + jnp.log(l), (B_BLOCK, HEAD_DIM))


def _dilated(src, cols, batch, seq, dilation):
    hpp = B_HEADS_PER_PAIR

    def spec(col0):
        return pl.BlockSpec((1, 1, seq, HEAD_DIM), lambda b, h: (col0 + h, b, 0, 0))

    out_spec = pl.BlockSpec((1, 1, seq, HEAD_DIM), lambda b, h: (h, b, 0, 0))
    return pl.pallas_call(
        functools.partial(_dilated_kernel, dilation=dilation, length=seq // dilation),
        grid=(batch, hpp),
        in_specs=[spec(cols[0]), spec(cols[1]), spec(cols[2])],
        out_specs=[out_spec, out_spec],
        out_shape=[jax.ShapeDtypeStruct((hpp, batch, seq, HEAD_DIM), BF16),
                   jax.ShapeDtypeStruct((hpp, batch, seq, HEAD_DIM), F32)],
        compiler_params=_cparams(("parallel", "parallel")),
        name=f"dilated{dilation}",
    )(src, src, src)


def _natten_kernel(q_ref, k_ref, v_ref, p_ref, o_ref, bias_ref, *, rows):
    n_blocks = rows // C_QROWS
    tq = C_QROWS * GRID_W
    tk = C_KROWS * GRID_W
    lane = lax.broadcasted_iota(jnp.int32, (GRID_W, 2 * GRID_W), 1)

    def window_start(block):
        return min(max(block * C_QROWS - NA_ROWS // 2, 0), rows - C_KROWS)

    def build_bias(slot, block):
        w0 = window_start(block)
        for a in range(C_QROWS):
            r = block * C_QROWS + a
            r0 = min(max(r - NA_ROWS // 2, 0), rows - NA_ROWS)
            for jp in range(C_KROWS // 2):
                kr = w0 + 2 * jp
                left_ok = r0 <= kr < r0 + NA_ROWS
                right_ok = r0 <= kr + 1 < r0 + NA_ROWS
                slab_index = min(max(kr - r + NA_ROWS, 0), 2 * NA_ROWS - 1)
                if left_ok and right_ok:
                    tile = p_ref[0, slab_index]
                elif left_ok:
                    tile = jnp.where(lane < GRID_W, p_ref[0, slab_index], NEG)
                elif right_ok:
                    tile = jnp.where(lane >= GRID_W, p_ref[0, slab_index], NEG)
                else:
                    tile = jnp.full((GRID_W, 2 * GRID_W), NEG, F32)
                bias_ref[slot, a * GRID_W:(a + 1) * GRID_W, jp * 2 * GRID_W:(jp + 1) * 2 * GRID_W] = tile

    build_bias(0, 0)
    build_bias(1, 1)
    build_bias(2, n_blocks - 1)

    def block(q0, k0, slot):
        q = q_ref[0, pl.ds(q0, tq), :]
        s = lax.dot_general(q, k_ref[0, pl.ds(k0, tk), :], (((1,), (1,)), ((), ())), preferred_element_type=F32)
        s = s + bias_ref[slot]
        m = jnp.max(s, axis=-1, keepdims=True)
        p = jnp.exp(s - m)
        l = jnp.sum(p, axis=-1, keepdims=True)
        o = jnp.dot(p.astype(BF16), v_ref[0, pl.ds(k0, tk), :], preferred_element_type=F32) * (1.0 / l)
        o_ref[0, pl.ds(q0, tq), :] = o.astype(BF16)

    for i in range(n_blocks):
        block(i * tq, window_start(i) * GRID_W, 0 if i == 0 else 2 if i == n_blocks - 1 else 1)


def _natten(main, bias_pairs, l, batch, seq):
    M = batch * seq
    rows = seq // GRID_W
    assert C_QROWS == NA_ROWS // 2 and C_KROWS == C_QROWS + NA_ROWS and rows % C_QROWS == 0

    def spec(col0):
        return pl.BlockSpec((1, seq, HEAD_DIM), lambda b, h: (col0 + h, b, 0))

    return pl.pallas_call(
        functools.partial(_natten_kernel, rows=rows),
        grid=(batch, C_HEADS),
        in_specs=[spec(MAIN_CQ), spec(MAIN_CK), spec(MAIN_CV),
                  pl.BlockSpec((None, 1, 2 * NA_ROWS, GRID_W, 2 * GRID_W), lambda b, h: (l, h, 0, 0, 0))],
        out_specs=pl.BlockSpec((1, seq, HEAD_DIM), lambda b, h: (h, b, 0)),
        out_shape=jax.ShapeDtypeStruct((C_HEADS, M, HEAD_DIM), BF16),
        scratch_shapes=[pltpu.VMEM((3, C_QROWS * GRID_W, C_KROWS * GRID_W), F32)],
        compiler_params=_cparams(("parallel", "parallel")),
        name="natten",
    )(main, main, main, bias_pairs)


def _layernorm(z, g, b):
    mu = jnp.mean(z, axis=-1, keepdims=True)
    zc = z - mu
    var = jnp.mean(zc * zc, axis=-1, keepdims=True)
    return zc * lax.rsqrt(var + EPS) * g + b


def _outproj_kernel(oa_ref, ob0_ref, ob1_ref, ob2_ref, l0_ref, l1_ref, l2_ref, oc_ref, x_ref, w_ref,
                    mg_ref, g_ref, b_ref, y_ref, ybf_ref, mixed_ref, tok_ref, *, alpha, sub):
    tm = x_ref.shape[0]
    n_tok = 0

    def token_major(ref, h, d):
        nonlocal n_tok
        slot = n_tok
        n_tok += 1
        for r in range(d):
            tok_ref[slot, pl.ds(r, tm // d, stride=d), :] = ref[h, 0, r].astype(F32)
        return slot

    slots = {}
    for h in range(B_HEADS_PER_PAIR):
        for g, (ob_ref, lse_ref) in ((1, (ob1_ref, l1_ref)), (2, (ob2_ref, l2_ref))):
            slots["o", g, h] = token_major(ob_ref, h, B_DILATIONS[g])
            slots["l", g, h] = token_major(lse_ref, h, B_DILATIONS[g])

    for s0 in range(0, tm, sub):
        rows = slice(s0, s0 + sub)

        def put_group(tiles, col0):
            ss = sum(jnp.sum(t * t, axis=-1, keepdims=True) for t in tiles)
            r = lax.rsqrt(ss * (1.0 / (len(tiles) * HEAD_DIM)) + EPS)
            for h, t in enumerate(tiles):
                lanes = slice((col0 + h) * HEAD_DIM, (col0 + h + 1) * HEAD_DIM)
                mixed_ref[rows, lanes] = (t * r * mg_ref[:, lanes]).astype(BF16)

        put_group([oa_ref[h, rows, :].astype(F32) for h in range(A_Q_HEADS)], 0)

        b_tiles = [None] * B_HEADS
        for h in range(B_HEADS_PER_PAIR):
            lses = [l0_ref[h, rows, :]] + [tok_ref[slots["l", g, h], rows, :] for g in (1, 2)]
            outs = [ob0_ref[h, rows, :].astype(F32)] + [tok_ref[slots["o", g, h], rows, :] for g in (1, 2)]
            mx = jnp.maximum(jnp.maximum(lses[0], lses[1]), lses[2])
            es = [jnp.exp(x - mx) for x in lses]
            inv = 1.0 / (es[0] + es[1] + es[2])
            for g in range(len(B_DILATIONS)):
                b_tiles[g * B_HEADS_PER_PAIR + h] = outs[g] * (es[g] * inv)
        put_group(b_tiles, A_Q_HEADS)

        put_group([oc_ref[h, rows, :].astype(F32) for h in range(C_HEADS)], A_Q_HEADS + B_HEADS)

        y = jnp.dot(mixed_ref[rows, :], w_ref[...], preferred_element_type=F32)
        y = _layernorm(alpha * x_ref[rows, :] + y, g_ref[...], b_ref[...])
        y_ref[rows, :] = y
        ybf_ref[rows, :] = y.astype(BF16)


def _outproj(oa, obs, lses, oc, x, w_out, mix_gain, ln_g, ln_b, l, alpha, batch, seq, *, tm=512, sub=256):
    M, D = x.shape
    nblk = seq // tm
    heads = lambda n: pl.BlockSpec((n, tm, HEAD_DIM), lambda i: (0, i, 0))
    hpp = B_HEADS_PER_PAIR

    def residue_major(a, d):
        view = a.reshape(hpp, batch, d, seq // d, HEAD_DIM)
        return view, pl.BlockSpec((hpp, 1, d, tm // d, HEAD_DIM), lambda i: (0, i // nblk, 0, i % nblk, 0))

    ob0 = obs[0].reshape(hpp, M, HEAD_DIM)
    l0 = lses[0].reshape(hpp, M, HEAD_DIM)
    (ob1, ob1_spec), (ob2, ob2_spec) = residue_major(obs[1], B_DILATIONS[1]), residue_major(obs[2], B_DILATIONS[2])
    (l1, l1_spec), (l2, l2_spec) = residue_major(lses[1], B_DILATIONS[1]), residue_major(lses[2], B_DILATIONS[2])
    vec = pl.BlockSpec((None, 1, D), lambda i: (l, 0, 0))
    return pl.pallas_call(
        functools.partial(_outproj_kernel, alpha=alpha, sub=sub),
        grid=(M // tm,),
        in_specs=[heads(A_Q_HEADS), heads(hpp), ob1_spec, ob2_spec, heads(hpp), l1_spec, l2_spec, heads(C_HEADS),
                  pl.BlockSpec((tm, D), lambda i: (i, 0)),
                  pl.BlockSpec((None, D, D), lambda i: (l, 0, 0), pipeline_mode=pl.Buffered(1)),
                  vec, vec, vec],
        out_specs=[pl.BlockSpec((tm, D), lambda i: (i, 0))] * 2,
        out_shape=[jax.ShapeDtypeStruct((M, D), F32), jax.ShapeDtypeStruct((M, D), BF16)],
        scratch_shapes=[pltpu.VMEM((tm, D), BF16), pltpu.VMEM((4 * hpp, tm, HEAD_DIM), F32)],
        compiler_params=_cparams(("parallel",)),
        name="outproj",
    )(oa, ob0, ob1, ob2, l0, l1, l2, oc, x, w_out, mix_gain, ln_g, ln_b)


def _mlp_kernel(x_ref, xbf_ref, w1_ref, w2_ref, g_ref, b_ref, y_ref, *, alpha):
    k = pl.program_id(1)

    @pl.when(k == 0)
    def _():
        y_ref[...] = jnp.zeros_like(y_ref)

    h = jnp.dot(xbf_ref[...], w1_ref[...], preferred_element_type=F32)
    h = jnp.square(jnp.maximum(h, 0.0))
    y_ref[...] += jnp.dot(h.astype(BF16), w2_ref[...], preferred_element_type=F32)

    @pl.when(k == pl.num_programs(1) - 1)
    def _():
        y_ref[...] = _layernorm(alpha * x_ref[...] + y_ref[...], g_ref[...], b_ref[...])


def _mlp(x, xbf, w1, w2, ln_g, ln_b, l, alpha, *, tm=512, tf=1024):
    M, D = x.shape
    F = w1.shape[-1]
    vec = pl.BlockSpec((None, 1, D), lambda i, k: (l, 0, 0))
    row_tile = pl.BlockSpec((tm, D), lambda i, k: (i, 0))
    return pl.pallas_call(
        functools.partial(_mlp_kernel, alpha=alpha),
        grid=(M // tm, F // tf),
        in_specs=[
            row_tile, row_tile,
            pl.BlockSpec((None, D, tf), lambda i, k: (l, 0, k)),
            pl.BlockSpec((None, tf, D), lambda i, k: (l, k, 0)),
            vec, vec,
        ],
        out_specs=row_tile,
        out_shape=jax.ShapeDtypeStruct((M, D), F32),
        compiler_params=_cparams(("parallel", "arbitrary")),
        name="mlp",
    )(x, xbf, w1, w2, ln_g, ln_b)


def _rope_tables(seq):
    t = jnp.arange(seq)

    def angles(pos, dim):
        inv = ROPE_THETA ** (-(jnp.arange(dim // 2, dtype=F32) * 2.0 / dim))
        return pos.astype(F32)[:, None] * inv[None, :]

    half = HEAD_DIM // 2
    ra, ca = angles(t // GRID_W, half), angles(t % GRID_W, half)
    zero = jnp.zeros_like(ra)
    cax = jnp.concatenate([jnp.cos(ra), jnp.cos(ra), jnp.cos(ca), jnp.cos(ca)], axis=-1)
    s1 = jnp.concatenate([-jnp.sin(ra), zero, -jnp.sin(ca), zero], axis=-1)
    s2 = jnp.concatenate([zero, jnp.sin(ra), zero, jnp.sin(ca)], axis=-1)
    ang = angles(t, HEAD_DIM)
    cb = jnp.concatenate([jnp.cos(ang), jnp.cos(ang)], axis=-1)
    sb = jnp.concatenate([-jnp.sin(ang), jnp.sin(ang)], axis=-1)
    return cax, s1, s2, cb, sb


def _bias_pairs(c_rel_bias):
    qc = jnp.arange(GRID_W)[:, None]
    kc = jnp.arange(GRID_W)[None, :]
    start = jnp.clip(qc - NA_COLS // 2, 0, GRID_W - NA_COLS)
    ok = (kc >= start) & (kc < start + NA_COLS)
    dcol = jnp.clip(kc - qc + NA_COLS - 1, 0, 2 * NA_COLS - 2)
    tile = jnp.where(ok[None, None, None], c_rel_bias[:, :, :, dcol], NEG)
    pad = jnp.full(tile.shape[:2] + (1,) + tile.shape[3:], NEG, F32)
    ext = jnp.concatenate([pad, tile, pad], axis=2)
    return jnp.concatenate([ext[:, :, :-1], ext[:, :, 1:]], axis=-1)


def _column_scale():
    cs = jnp.ones((N_COLS, HEAD_DIM), F32)
    for c0, n in ((COL_BQ, B_HEADS), (COL_CQ, C_HEADS)):
        cs = cs.at[c0:c0 + n].set(SCALE)
    return cs.reshape(1, N_COLS * HEAD_DIM)


def _trunk(x, params, tabs, colscale, depth, alpha):
    batch, seq, d_model = x.shape
    hpp = B_HEADS_PER_PAIR
    x = x.reshape(batch * seq, d_model)
    for l in range(depth):
        main, d4, d16 = _inproj(x, params["w_in"], l, colscale, params["gq"], params["gk"], tabs, batch, seq)
        oa = _gqa(main, batch, seq)
        srcs = (main.reshape(N_MAIN, batch, seq, HEAD_DIM), d4.reshape(N_PAIR_COLS, batch, seq, HEAD_DIM),
                d16.reshape(N_PAIR_COLS, batch, seq, HEAD_DIM))
        cols = ((MAIN_BQ, MAIN_BK, MAIN_BV), (0, hpp, 2 * hpp), (0, hpp, 2 * hpp))
        obs, lses = zip(*[_dilated(srcs[g], cols[g], batch, seq, B_DILATIONS[g]) for g in range(len(B_DILATIONS))])
        oc = _natten(main, params["bias_pairs"], l, batch, seq)
        x, xbf = _outproj(oa, obs, lses, oc, x, params["w_out"], params["mix_gain"], params["ln1_g"],
                          params["ln1_b"], l, alpha, batch, seq)
        x = _mlp(x, xbf, params["w_ff1"], params["w_ff2"], params["ln2_g"], params["ln2_b"], l, alpha)
    return x.reshape(batch, seq, d_model)


def _prepare(w_in, a_q_gain, a_k_gain, c_rel_bias, mix_gain, w_out, ln1_g, ln1_b, w_ff1, w_ff2, ln2_g, ln2_b):
    depth = w_in.shape[0]
    vec = lambda a: a.reshape(depth, 1, a.shape[-1])
    return {
        "w_in": w_in.astype(BF16), "w_out": w_out.astype(BF16),
        "w_ff1": w_ff1.astype(BF16), "w_ff2": w_ff2.astype(BF16),
        "gq": vec(a_q_gain * SCALE), "gk": vec(a_k_gain),
        "bias_pairs": _bias_pairs(c_rel_bias),
        "mix_gain": vec(mix_gain), "ln1_g": vec(ln1_g), "ln1_b": vec(ln1_b),
        "ln2_g": vec(ln2_g), "ln2_b": vec(ln2_b),
    }


def kernel(x_prompt, x_sample, w_in, a_q_gain, a_k_gain, c_rel_bias, mix_gain, w_out, ln1_g, ln1_b,
           w_ff1, w_ff2, ln2_g, ln2_b):
    depth = w_in.shape[0]
    alpha = (2 * depth) ** 0.25
    params = _prepare(w_in, a_q_gain, a_k_gain, c_rel_bias, mix_gain, w_out, ln1_g, ln1_b, w_ff1, w_ff2,
                      ln2_g, ln2_b)
    tabs = _rope_tables(x_prompt.shape[1])
    colscale = _column_scale()
    y_prompt = _trunk(x_prompt, params, tabs, colscale, depth, alpha)
    y_sample = _trunk(x_sample, params, tabs, colscale, depth, alpha)
    return (y_prompt, y_sample)
```

```python
import functools

import jax
import jax.numpy as jnp
from jax import lax
from jax.experimental import pallas as pl
from jax.experimental.pallas import tpu as pltpu

F32 = jnp.float32
BF16 = jnp.bfloat16

HEAD_DIM = 128
A_Q_HEADS = 4
A_KV_HEADS = 2
B_DILATIONS = (1, 4, 16)
B_HEADS_PER_PAIR = 2
B_HEADS = 6
C_HEADS = 6
GRID_W = 64
NA_ROWS = 8
NA_COLS = 16
ROPE_THETA = 10000.0
EPS = 1e-6
NEG = -1e30
SCALE = HEAD_DIM ** -0.5
LOG2E = 1.4426950408889634

COL_AQ, COL_AK, COL_AV = 0, 4, 6
COL_BQ, COL_BK, COL_BV = 8, 14, 20
COL_CQ, COL_CK, COL_CV = 26, 32, 38
N_COLS = 44

MAIN_AQ, MAIN_AK, MAIN_AV = 0, 4, 6
MAIN_BQ, MAIN_BK, MAIN_BV = 8, 10, 12
MAIN_CQ, MAIN_CK, MAIN_CV = 14, 20, 26
N_MAIN = 32
N_PAIR_COLS = 3 * B_HEADS_PER_PAIR


def _column_destination(col):
    hpp = B_HEADS_PER_PAIR
    for ref0, main0, n in ((COL_AQ, MAIN_AQ, 4), (COL_AK, MAIN_AK, 2), (COL_AV, MAIN_AV, 2),
                           (COL_CQ, MAIN_CQ, 6), (COL_CK, MAIN_CK, 6), (COL_CV, MAIN_CV, 6)):
        if ref0 <= col < ref0 + n:
            return 0, main0 + col - ref0
    for which, (ref0, main0) in enumerate(((COL_BQ, MAIN_BQ), (COL_BK, MAIN_BK), (COL_BV, MAIN_BV))):
        if ref0 <= col < ref0 + B_HEADS:
            pair, h = divmod(col - ref0, hpp)
            if pair == 0:
                return 0, main0 + h
            return pair, which * hpp + h
    raise ValueError(col)


B_BLOCK = 128
B_SIDE = 64
B_KWIN = B_BLOCK + 2 * B_SIDE
C_QROWS = 4
C_KROWS = 12
A_ROWS = 256
A_KCHUNK = 512

VMEM_LIMIT = 56 * 1024 * 1024


def _cparams(sem):
    return pltpu.CompilerParams(dimension_semantics=sem, vmem_limit_bytes=VMEM_LIMIT)


def _inproj_kernel(x_ref, w_ref, cs_ref, gq_ref, gk_ref, cax_ref, s1_ref, s2_ref, cb_ref, sb_ref,
                   main_ref, d4_ref, d16_ref, xbf_ref, perm_ref, *, tn):
    xbf_ref[...] = x_ref[...].astype(BF16)
    heads_per_tile = tn // HEAD_DIM
    tm = x_ref.shape[0]
    pair_refs = {1: d4_ref, 2: d16_ref}
    n_perm = 0

    def axial(xh, gain):
        ms = jnp.mean(xh * xh, axis=-1, keepdims=True)
        y = xh * lax.rsqrt(ms + EPS) * gain
        return y * cax_ref[...] + pltpu.roll(y, 96, 1) * s1_ref[...] + pltpu.roll(y, 32, 1) * s2_ref[...]

    def rope(xh):
        return xh * cb_ref[...] + pltpu.roll(xh, 64, 1) * sb_ref[...]

    for t in range(N_COLS // heads_per_tile):
        acc = jnp.dot(xbf_ref[...], w_ref[:, t * tn:(t + 1) * tn], preferred_element_type=F32)
        acc = acc * cs_ref[:, t * tn:(t + 1) * tn]
        for c in range(heads_per_tile):
            col = t * heads_per_tile + c
            xh = acc[:, c * HEAD_DIM:(c + 1) * HEAD_DIM]
            if COL_AQ <= col < COL_AK:
                xh = axial(xh, gq_ref[...])
            elif COL_AK <= col < COL_AV:
                xh = axial(xh, gk_ref[...])
            elif COL_BQ <= col < COL_BV:
                xh = rope(xh)
            dest, idx = _column_destination(col)
            if dest == 0:
                main_ref[idx] = xh.astype(BF16)
            else:
                d = B_DILATIONS[dest]
                slot = n_perm % perm_ref.shape[0]
                n_perm += 1
                perm_ref[slot] = xh
                for r in range(d):
                    pair_refs[dest][idx, 0, r] = perm_ref[slot, pl.ds(r, tm // d, stride=d), :].astype(BF16)


def _inproj(x, w_in, l, colscale, gq, gk, tabs, batch, seq, *, tm=512, tn=512):
    M, D = x.shape
    nblk = seq // tm
    tab_spec = pl.BlockSpec((tm, HEAD_DIM), lambda i: (i % nblk, 0))

    def pair_out(d):
        shape = (N_PAIR_COLS, batch, d, seq // d, HEAD_DIM)
        spec = pl.BlockSpec((N_PAIR_COLS, 1, d, tm // d, HEAD_DIM), lambda i: (0, i // nblk, 0, i % nblk, 0))
        return jax.ShapeDtypeStruct(shape, BF16), spec

    (d4_shape, d4_spec), (d16_shape, d16_spec) = pair_out(B_DILATIONS[1]), pair_out(B_DILATIONS[2])
    return pl.pallas_call(
        functools.partial(_inproj_kernel, tn=tn),
        grid=(M // tm,),
        in_specs=[
            pl.BlockSpec((tm, D), lambda i: (i, 0)),
            pl.BlockSpec((None, D, N_COLS * HEAD_DIM), lambda i: (l, 0, 0), pipeline_mode=pl.Buffered(1)),
            pl.BlockSpec((1, N_COLS * HEAD_DIM), lambda i: (0, 0)),
            pl.BlockSpec((None, 1, HEAD_DIM), lambda i: (l, 0, 0)),
            pl.BlockSpec((None, 1, HEAD_DIM), lambda i: (l, 0, 0)),
            tab_spec, tab_spec, tab_spec, tab_spec, tab_spec,
        ],
        out_specs=[pl.BlockSpec((N_MAIN, tm, HEAD_DIM), lambda i: (0, i, 0)), d4_spec, d16_spec],
        out_shape=[jax.ShapeDtypeStruct((N_MAIN, M, HEAD_DIM), BF16), d4_shape, d16_shape],
        scratch_shapes=[pltpu.VMEM((tm, D), BF16), pltpu.VMEM((2, tm, HEAD_DIM), F32)],
        compiler_params=_cparams(("parallel",)),
        name="inproj",
    )(x, w_in, colscale, gq, gk, *tabs)


def _lane_groups(x):
    return [x[:, g * HEAD_DIM:(g + 1) * HEAD_DIM] for g in range(x.shape[1] // HEAD_DIM)]


def _with_ones(vext_ref, v):
    vext_ref[:, :HEAD_DIM] = v
    vext_ref[:, HEAD_DIM:] = jnp.ones(v.shape, v.dtype)


def _gqa_kernel(q_ref, k_ref, v_ref, o_ref, s_ref, m_ref, vext_ref):
    rep, seq, _ = q_ref.shape
    _with_ones(vext_ref, v_ref[0])
    blocks_per_head = seq // A_ROWS
    n_blocks = rep * blocks_per_head
    n_chunks = seq // A_KCHUNK
    chunks = [slice(c * A_KCHUNK, (c + 1) * A_KCHUNK) for c in range(n_chunks)]

    def rows_of(i):
        if isinstance(i, int):
            return i // blocks_per_head, pl.ds((i % blocks_per_head) * A_ROWS, A_ROWS)
        return i // blocks_per_head, pl.ds(pl.multiple_of((i % blocks_per_head) * A_ROWS, A_ROWS), A_ROWS)

    def scores(i, slot):
        h, rows = rows_of(i)
        q = q_ref[h, rows, :]
        mx = None
        for keys in chunks:
            s = lax.dot_general(q, k_ref[0, keys, :], (((1,), (1,)), ((), ())), preferred_element_type=F32)
            s_ref[slot, :, keys] = s
            part = functools.reduce(jnp.maximum, _lane_groups(s))
            mx = part if mx is None else jnp.maximum(mx, part)
        m_ref[slot] = jnp.broadcast_to(jnp.max(mx, axis=-1, keepdims=True), (A_ROWS, HEAD_DIM))

    def finish(i, slot):
        h, rows = rows_of(i)
        m = m_ref[slot]
        ps = []
        for keys in chunks:
            ps += [jnp.exp2(s - m).astype(BF16) for s in _lane_groups(s_ref[slot, :, keys])]
        r = jnp.dot(jnp.concatenate(ps, axis=1), vext_ref[...], preferred_element_type=F32)
        o_ref[h, rows, :] = (r[:, :HEAD_DIM] * (1.0 / r[:, HEAD_DIM:])).astype(BF16)

    scores(0, 0)

    def body(j, carry):
        scores(2 * j + 1, 1)
        finish(2 * j, 0)
        scores(2 * j + 2, 0)
        finish(2 * j + 1, 1)
        return carry

    lax.fori_loop(0, n_blocks // 2 - 1, body, 0)
    scores(n_blocks - 1, 1)
    finish(n_blocks - 2, 0)
    finish(n_blocks - 1, 1)


def _gqa(main, batch, seq):
    M = batch * seq
    rep = A_Q_HEADS // A_KV_HEADS
    return pl.pallas_call(
        _gqa_kernel,
        grid=(batch, A_KV_HEADS),
        in_specs=[
            pl.BlockSpec((rep, seq, HEAD_DIM), lambda b, g: (g, b, 0)),
            pl.BlockSpec((1, seq, HEAD_DIM), lambda b, g: (MAIN_AK + g, b, 0)),
            pl.BlockSpec((1, seq, HEAD_DIM), lambda b, g: (MAIN_AV + g, b, 0)),
        ],
        out_specs=pl.BlockSpec((rep, seq, HEAD_DIM), lambda b, g: (g, b, 0)),
        out_shape=jax.ShapeDtypeStruct((A_Q_HEADS, M, HEAD_DIM), BF16),
        scratch_shapes=[pltpu.VMEM((2, A_ROWS, seq), F32), pltpu.VMEM((2, A_ROWS, HEAD_DIM), F32),
                        pltpu.VMEM((seq, 2 * HEAD_DIM), BF16)],
        compiler_params=_cparams(("parallel", "parallel")),
        name="gqa",
    )(main, main, main)


def _dilated_kernel(q_ref, k_ref, v_ref, o_ref, lse_ref, *, dilation, length):
    rel0 = (lax.broadcasted_iota(jnp.int32, (B_BLOCK, B_KWIN), 1)
            - lax.broadcasted_iota(jnp.int32, (B_BLOCK, B_KWIN), 0))
    masks = {}
    for r in range(dilation):
        for qb in range(length // B_BLOCK):
            q0 = qb * B_BLOCK
            k0 = min(max(q0 - B_SIDE, 0), length - B_KWIN)
            if k0 - q0 not in masks:
                rel = rel0 + (k0 - q0)
                masks[k0 - q0] = (rel >= -B_SIDE) & (rel <= B_SIDE)
            q_rows = pl.ds(r * length + q0, B_BLOCK)
            k_rows = pl.ds(r * length + k0, B_KWIN)
            s = lax.dot_general(q_ref[0, 0, q_rows, :], k_ref[0, 0, k_rows, :], (((1,), (1,)), ((), ())),
                                preferred_element_type=F32)
            s = jnp.where(masks[k0 - q0], s, NEG)
            m = jnp.max(s, axis=-1, keepdims=True)
            p = jnp.exp(s - m)
            l = jnp.sum(p, axis=-1, keepdims=True)
            o = jnp.dot(p.astype(BF16), v_ref[0, 0, k_rows, :], preferred_element_type=F32) * (1.0 / l)
            o_ref[0, 0, q_rows, :] = o.astype(BF16)
            lse_ref[0, 0, q_rows, :] = jnp.broadcast_to(m + jnp.log(l), (B_BLOCK, HEAD_DIM))


def _dilated(src, cols, batch, seq, dilation):
    hpp = B_HEADS_PER_PAIR

    def spec(col0):
        return pl.BlockSpec((1, 1, seq, HEAD_DIM), lambda b, h: (col0 + h, b, 0, 0))

    out_spec = pl.BlockSpec((1, 1, seq, HEAD_DIM), lambda b, h: (h, b, 0, 0))
    return pl.pallas_call(
        functools.partial(_dilated_kernel, dilation=dilation, length=seq // dilation),
        grid=(batch, hpp),
        in_specs=[spec(cols[0]), spec(cols[1]), spec(cols[2])],
        out_specs=[out_spec, out_spec],
        out_shape=[jax.ShapeDtypeStruct((hpp, batch, seq, HEAD_DIM), BF16),
                   jax.ShapeDtypeStruct((hpp, batch, seq, HEAD_DIM), F32)],
        compiler_params=_cparams(("parallel", "parallel")),
        name=f"dilated{dilation}",
    )(src, src, src)


def _natten_kernel(q_ref, k_ref, v_ref, p_ref, o_ref, bias_ref, vext_ref, *, rows):
    _with_ones(vext_ref, v_ref[0])
    n_blocks = rows // C_QROWS
    tq = C_QROWS * GRID_W
    tk = C_KROWS * GRID_W
    lane = lax.broadcasted_iota(jnp.int32, (GRID_W, 2 * GRID_W), 1)

    def window_start(block):
        return min(max(block * C_QROWS - NA_ROWS // 2, 0), rows - C_KROWS)

    def build_bias(slot, block):
        w0 = window_start(block)
        for a in range(C_QROWS):
            r = block * C_QROWS + a
            r0 = min(max(r - NA_ROWS // 2, 0), rows - NA_ROWS)
            for jp in range(C_KROWS // 2):
                kr = w0 + 2 * jp
                left_ok = r0 <= kr < r0 + NA_ROWS
                right_ok = r0 <= kr + 1 < r0 + NA_ROWS
                slab_index = min(max(kr - r + NA_ROWS, 0), 2 * NA_ROWS - 1)
                if left_ok and right_ok:
                    tile = p_ref[0, slab_index]
                elif left_ok:
                    tile = jnp.where(lane < GRID_W, p_ref[0, slab_index], NEG)
                elif right_ok:
                    tile = jnp.where(lane >= GRID_W, p_ref[0, slab_index], NEG)
                else:
                    tile = jnp.full((GRID_W, 2 * GRID_W), NEG, F32)
                bias_ref[slot, a * GRID_W:(a + 1) * GRID_W, jp * 2 * GRID_W:(jp + 1) * 2 * GRID_W] = tile

    build_bias(0, 0)
    build_bias(1, 1)
    build_bias(2, n_blocks - 1)

    def block(q0, k0, slot):
        q = q_ref[0, pl.ds(q0, tq), :]
        s = lax.dot_general(q, k_ref[0, pl.ds(k0, tk), :], (((1,), (1,)), ((), ())), preferred_element_type=F32)
        s = s + bias_ref[slot]
        m = jnp.max(s, axis=-1, keepdims=True)
        p = jnp.exp2(s - m).astype(BF16)
        r = jnp.dot(p, vext_ref[pl.ds(k0, tk), :], preferred_element_type=F32)
        o_ref[0, pl.ds(q0, tq), :] = (r[:, :HEAD_DIM] * (1.0 / r[:, HEAD_DIM:])).astype(BF16)

    for i in range(n_blocks):
        block(i * tq, window_start(i) * GRID_W, 0 if i == 0 else 2 if i == n_blocks - 1 else 1)


def _natten(main, bias_pairs, l, batch, seq):
    M = batch * seq
    rows = seq // GRID_W
    assert C_QROWS == NA_ROWS // 2 and C_KROWS == C_QROWS + NA_ROWS and rows % C_QROWS == 0

    def spec(col0):
        return pl.BlockSpec((1, seq, HEAD_DIM), lambda b, h: (col0 + h, b, 0))

    return pl.pallas_call(
        functools.partial(_natten_kernel, rows=rows),
        grid=(batch, C_HEADS),
        in_specs=[spec(MAIN_CQ), spec(MAIN_CK), spec(MAIN_CV),
                  pl.BlockSpec((None, 1, 2 * NA_ROWS, GRID_W, 2 * GRID_W), lambda b, h: (l, h, 0, 0, 0))],
        out_specs=pl.BlockSpec((1, seq, HEAD_DIM), lambda b, h: (h, b, 0)),
        out_shape=jax.ShapeDtypeStruct((C_HEADS, M, HEAD_DIM), BF16),
        scratch_shapes=[pltpu.VMEM((3, C_QROWS * GRID_W, C_KROWS * GRID_W), F32),
                        pltpu.VMEM((seq, 2 * HEAD_DIM), BF16)],
        compiler_params=_cparams(("parallel", "parallel")),
        name="natten",
    )(main, main, main, bias_pairs)


def _layernorm(z, g, b):
    mu = jnp.mean(z, axis=-1, keepdims=True)
    zc = z - mu
    var = jnp.mean(zc * zc, axis=-1, keepdims=True)
    return zc * lax.rsqrt(var + EPS) * g + b


def _outproj_kernel(oa_ref, ob0_ref, ob1_ref, ob2_ref, l0_ref, l1_ref, l2_ref, oc_ref, x_ref, w_ref,
                    mg_ref, g_ref, b_ref, y_ref, ybf_ref, mixed_ref, tok_ref, *, alpha, sub):
    tm = x_ref.shape[0]
    n_tok = 0

    def token_major(ref, h, d):
        nonlocal n_tok
        slot = n_tok
        n_tok += 1
        for r in range(d):
            tok_ref[slot, pl.ds(r, tm // d, stride=d), :] = ref[h, 0, r].astype(F32)
        return slot

    slots = {}
    for h in range(B_HEADS_PER_PAIR):
        for g, (ob_ref, lse_ref) in ((1, (ob1_ref, l1_ref)), (2, (ob2_ref, l2_ref))):
            slots["o", g, h] = token_major(ob_ref, h, B_DILATIONS[g])
            slots["l", g, h] = token_major(lse_ref, h, B_DILATIONS[g])

    for s0 in range(0, tm, sub):
        rows = slice(s0, s0 + sub)

        def put_group(tiles, col0):
            ss = sum(jnp.sum(t * t, axis=-1, keepdims=True) for t in tiles)
            r = lax.rsqrt(ss * (1.0 / (len(tiles) * HEAD_DIM)) + EPS)
            for h, t in enumerate(tiles):
                lanes = slice((col0 + h) * HEAD_DIM, (col0 + h + 1) * HEAD_DIM)
                mixed_ref[rows, lanes] = (t * r * mg_ref[:, lanes]).astype(BF16)

        put_group([oa_ref[h, rows, :].astype(F32) for h in range(A_Q_HEADS)], 0)

        b_tiles = [None] * B_HEADS
        for h in range(B_HEADS_PER_PAIR):
            lses = [l0_ref[h, rows, :]] + [tok_ref[slots["l", g, h], rows, :] for g in (1, 2)]
            outs = [ob0_ref[h, rows, :].astype(F32)] + [tok_ref[slots["o", g, h], rows, :] for g in (1, 2)]
            mx = jnp.maximum(jnp.maximum(lses[0], lses[1]), lses[2])
            es = [jnp.exp(x - mx) for x in lses]
            inv = 1.0 / (es[0] + es[1] + es[2])
            for g in range(len(B_DILATIONS)):
                b_tiles[g * B_HEADS_PER_PAIR + h] = outs[g] * (es[g] * inv)
        put_group(b_tiles, A_Q_HEADS)

        put_group([oc_ref[h, rows, :].astype(F32) for h in range(C_HEADS)], A_Q_HEADS + B_HEADS)

        y = jnp.dot(mixed_ref[rows, :], w_ref[...], preferred_element_type=F32)
        y = _layernorm(alpha * x_ref[rows, :] + y, g_ref[...], b_ref[...])
        y_ref[rows, :] = y
        ybf_ref[rows, :] = y.astype(BF16)


def _outproj(oa, obs, lses, oc, x, w_out, mix_gain, ln_g, ln_b, l, alpha, batch, seq, *, tm=512, sub=256):
    M, D = x.shape
    nblk = seq // tm
    heads = lambda n: pl.BlockSpec((n, tm, HEAD_DIM), lambda i: (0, i, 0))
    hpp = B_HEADS_PER_PAIR

    def residue_major(a, d):
        view = a.reshape(hpp, batch, d, seq // d, HEAD_DIM)
        return view, pl.BlockSpec((hpp, 1, d, tm // d, HEAD_DIM), lambda i: (0, i // nblk, 0, i % nblk, 0))

    ob0 = obs[0].reshape(hpp, M, HEAD_DIM)
    l0 = lses[0].reshape(hpp, M, HEAD_DIM)
    (ob1, ob1_spec), (ob2, ob2_spec) = residue_major(obs[1], B_DILATIONS[1]), residue_major(obs[2], B_DILATIONS[2])
    (l1, l1_spec), (l2, l2_spec) = residue_major(lses[1], B_DILATIONS[1]), residue_major(lses[2], B_DILATIONS[2])
    vec = pl.BlockSpec((None, 1, D), lambda i: (l, 0, 0))
    return pl.pallas_call(
        functools.partial(_outproj_kernel, alpha=alpha, sub=sub),
        grid=(M // tm,),
        in_specs=[heads(A_Q_HEADS), heads(hpp), ob1_spec, ob2_spec, heads(hpp), l1_spec, l2_spec, heads(C_HEADS),
                  pl.BlockSpec((tm, D), lambda i: (i, 0)),
                  pl.BlockSpec((None, D, D), lambda i: (l, 0, 0), pipeline_mode=pl.Buffered(1)),
                  vec, vec, vec],
        out_specs=[pl.BlockSpec((tm, D), lambda i: (i, 0))] * 2,
        out_shape=[jax.ShapeDtypeStruct((M, D), F32), jax.ShapeDtypeStruct((M, D), BF16)],
        scratch_shapes=[pltpu.VMEM((tm, D), BF16), pltpu.VMEM((4 * hpp, tm, HEAD_DIM), F32)],
        compiler_params=_cparams(("parallel",)),
        name="outproj",
    )(oa, ob0, ob1, ob2, l0, l1, l2, oc, x, w_out, mix_gain, ln_g, ln_b)


def _mlp_kernel(x_ref, xbf_ref, w1_ref, w2_ref, g_ref, b_ref, y_ref, *, alpha):
    k = pl.program_id(1)

    @pl.when(k == 0)
    def _():
        y_ref[...] = jnp.zeros_like(y_ref)

    h = jnp.dot(xbf_ref[...], w1_ref[...], preferred_element_type=F32)
    h = jnp.square(jnp.maximum(h, 0.0))
    y_ref[...] += jnp.dot(h.astype(BF16), w2_ref[...], preferred_element_type=F32)

    @pl.when(k == pl.num_programs(1) - 1)
    def _():
        y_ref[...] = _layernorm(alpha * x_ref[...] + y_ref[...], g_ref[...], b_ref[...])


def _mlp(x, xbf, w1, w2, ln_g, ln_b, l, alpha, *, tm=512, tf=1024):
    M, D = x.shape
    F = w1.shape[-1]
    vec = pl.BlockSpec((None, 1, D), lambda i, k: (l, 0, 0))
    row_tile = pl.BlockSpec((tm, D), lambda i, k: (i, 0))
    return pl.pallas_call(
        functools.partial(_mlp_kernel, alpha=alpha),
        grid=(M // tm, F // tf),
        in_specs=[
            row_tile, row_tile,
            pl.BlockSpec((None, D, tf), lambda i, k: (l, 0, k)),
            pl.BlockSpec((None, tf, D), lambda i, k: (l, k, 0)),
            vec, vec,
        ],
        out_specs=row_tile,
        out_shape=jax.ShapeDtypeStruct((M, D), F32),
        compiler_params=_cparams(("parallel", "arbitrary")),
        name="mlp",
    )(x, xbf, w1, w2, ln_g, ln_b)


def _rope_tables(seq):
    t = jnp.arange(seq)

    def angles(pos, dim):
        inv = ROPE_THETA ** (-(jnp.arange(dim // 2, dtype=F32) * 2.0 / dim))
        return pos.astype(F32)[:, None] * inv[None, :]

    half = HEAD_DIM // 2
    ra, ca = angles(t // GRID_W, half), angles(t % GRID_W, half)
    zero = jnp.zeros_like(ra)
    cax = jnp.concatenate([jnp.cos(ra), jnp.cos(ra), jnp.cos(ca), jnp.cos(ca)], axis=-1)
    s1 = jnp.concatenate([-jnp.sin(ra), zero, -jnp.sin(ca), zero], axis=-1)
    s2 = jnp.concatenate([zero, jnp.sin(ra), zero, jnp.sin(ca)], axis=-1)
    ang = angles(t, HEAD_DIM)
    cb = jnp.concatenate([jnp.cos(ang), jnp.cos(ang)], axis=-1)
    sb = jnp.concatenate([-jnp.sin(ang), jnp.sin(ang)], axis=-1)
    return cax, s1, s2, cb, sb


def _bias_pairs(c_rel_bias):
    qc = jnp.arange(GRID_W)[:, None]
    kc = jnp.arange(GRID_W)[None, :]
    start = jnp.clip(qc - NA_COLS // 2, 0, GRID_W - NA_COLS)
    ok = (kc >= start) & (kc < start + NA_COLS)
    dcol = jnp.clip(kc - qc + NA_COLS - 1, 0, 2 * NA_COLS - 2)
    tile = jnp.where(ok[None, None, None], c_rel_bias[:, :, :, dcol] * LOG2E, NEG)
    pad = jnp.full(tile.shape[:2] + (1,) + tile.shape[3:], NEG, F32)
    ext = jnp.concatenate([pad, tile, pad], axis=2)
    return jnp.concatenate([ext[:, :, :-1], ext[:, :, 1:]], axis=-1)


def _column_scale():
    cs = jnp.ones((N_COLS, HEAD_DIM), F32)
    cs = cs.at[COL_BQ:COL_BQ + B_HEADS].set(SCALE)
    cs = cs.at[COL_CQ:COL_CQ + C_HEADS].set(SCALE * LOG2E)
    return cs.reshape(1, N_COLS * HEAD_DIM)


def _trunk(x, params, tabs, colscale, depth, alpha):
    batch, seq, d_model = x.shape
    hpp = B_HEADS_PER_PAIR
    x = x.reshape(batch * seq, d_model)
    for l in range(depth):
        main, d4, d16 = _inproj(x, params["w_in"], l, colscale, params["gq"], params["gk"], tabs, batch, seq)
        oa = _gqa(main, batch, seq)
        srcs = (main.reshape(N_MAIN, batch, seq, HEAD_DIM), d4.reshape(N_PAIR_COLS, batch, seq, HEAD_DIM),
                d16.reshape(N_PAIR_COLS, batch, seq, HEAD_DIM))
        cols = ((MAIN_BQ, MAIN_BK, MAIN_BV), (0, hpp, 2 * hpp), (0, hpp, 2 * hpp))
        obs, lses = zip(*[_dilated(srcs[g], cols[g], batch, seq, B_DILATIONS[g]) for g in range(len(B_DILATIONS))])
        oc = _natten(main, params["bias_pairs"], l, batch, seq)
        x, xbf = _outproj(oa, obs, lses, oc, x, params["w_out"], params["mix_gain"], params["ln1_g"],
                          params["ln1_b"], l, alpha, batch, seq)
        x = _mlp(x, xbf, params["w_ff1"], params["w_ff2"], params["ln2_g"], params["ln2_b"], l, alpha)
    return x.reshape(batch, seq, d_model)


def _prepare(w_in, a_q_gain, a_k_gain, c_rel_bias, mix_gain, w_out, ln1_g, ln1_b, w_ff1, w_ff2, ln2_g, ln2_b):
    depth = w_in.shape[0]
    vec = lambda a: a.reshape(depth, 1, a.shape[-1])
    return {
        "w_in": w_in.astype(BF16), "w_out": w_out.astype(BF16),
        "w_ff1": w_ff1.astype(BF16), "w_ff2": w_ff2.astype(BF16),
        "gq": vec(a_q_gain * (SCALE * LOG2E)), "gk": vec(a_k_gain),
        "bias_pairs": _bias_pairs(c_rel_bias),
        "mix_gain": vec(mix_gain), "ln1_g": vec(ln1_g), "ln1_b": vec(ln1_b),
        "ln2_g": vec(ln2_g), "ln2_b": vec(ln2_b),
    }


def kernel(x_prompt, x_sample, w_in, a_q_gain, a_k_gain, c_rel_bias, mix_gain, w_out, ln1_g, ln1_b,
           w_ff1, w_ff2, ln2_g, ln2_b):
    depth = w_in.shape[0]
    alpha = (2 * depth) ** 0.25
    params = _prepare(w_in, a_q_gain, a_k_gain, c_rel_bias, mix_gain, w_out, ln1_g, ln1_b, w_ff1, w_ff2,
                      ln2_g, ln2_b)
    tabs = _rope_tables(x_prompt.shape[1])
    colscale = _column_scale()
    y_prompt = _trunk(x_prompt, params, tabs, colscale, depth, alpha)
    y_sample = _trunk(x_sample, params, tabs, colscale, depth, alpha)
    return (y_prompt, y_sample)
```
